```python
import jax, jax.numpy as jnp
from jax import lax
import numpy as np

D_MODEL = 1024
BATCH = 1
SEQ = 16384
DEPTH = 4

GRID_W = 64
CTX_LEN = 256
RET_HEADS = 4
RET_QK_HEAD = 128
RET_V_HEAD = 256
RET_QK_WIDTH = RET_HEADS * RET_QK_HEAD
RET_V_WIDTH = RET_HEADS * RET_V_HEAD
CONV_WIDTH = 1024
CONV_TAPS = 3
D_FF = 2816
CHUNK = 128
N_MOD = 9
ROPE_BASE = 10000.0
NORM_EPS = 1e-6
IN_SECTIONS = (RET_QK_WIDTH, RET_V_WIDTH, RET_QK_WIDTH, RET_V_WIDTH, CONV_WIDTH, CONV_WIDTH, CONV_WIDTH, D_MODEL, D_MODEL)
IN_WIDTH = 2 * RET_QK_WIDTH + 2 * RET_V_WIDTH + 3 * CONV_WIDTH + 2 * D_MODEL
KV_WIDTH = RET_QK_WIDTH + RET_V_WIDTH

kernel_name = "hybrid_retention_shortconv_macaron_dit"


def rmsnorm(h, w):
    hf = h.astype(jnp.float32)
    hf = hf * lax.rsqrt(jnp.mean(hf * hf, axis=-1, keepdims=True) + NORM_EPS)
    return (hf * w.astype(jnp.float32)).astype(h.dtype)


def head_rmsnorm(o):
    return o * lax.rsqrt(jnp.mean(o * o, axis=-1, keepdims=True) + NORM_EPS)


def adaln_params(cond, w_mod, b_mod):
    m = jax.nn.silu(cond) @ w_mod + b_mod
    return m.reshape(cond.shape[0], 1, N_MOD, D_MODEL)


def modulated_norm(h, w, mod, i):
    return rmsnorm(h, w) * (1.0 + mod[:, :, 3 * i + 1]) + mod[:, :, 3 * i]


def swiglu(y, w_up, w_down):
    a, b = jnp.split(y @ w_up, 2, axis=-1)
    return (jax.nn.silu(a) * b) @ w_down


def ffn_sublayer(h, mod, i, norm_w, w_up, w_down):
    y = modulated_norm(h, norm_w, mod, i)
    return h + 0.5 * mod[:, :, 3 * i + 2] * swiglu(y, w_up, w_down)


def split_projection(p):
    pieces = []
    start = 0
    for width in IN_SECTIONS:
        pieces.append(p[..., start:start + width])
        start += width
    return pieces


def to_heads(t):
    b, n, w = t.shape
    return t.reshape(b, n, RET_HEADS, w // RET_HEADS).transpose(0, 2, 1, 3)


def axial_rope(t, pos_row, pos_col):
    half = t.shape[-1] // 2
    n_freq = half // 2
    inv_freq = ROPE_BASE ** (-jnp.arange(n_freq, dtype=jnp.float32) / n_freq)

    def rotate(part, pos):
        ang = pos.astype(jnp.float32)[:, None] * inv_freq[None, :]
        cos, sin = jnp.cos(ang).astype(t.dtype), jnp.sin(ang).astype(t.dtype)
        a, b = part[..., :n_freq], part[..., n_freq:]
        return jnp.concatenate([a * cos - b * sin, a * sin + b * cos], axis=-1)

    return jnp.concatenate([rotate(t[..., :half], pos_row), rotate(t[..., half:], pos_col)], axis=-1)


def retention_chunked(q, k, v, log_gamma, s0):
    batch, heads, n, dk = q.shape
    dv = v.shape[-1]
    nc = n // CHUNK
    qf = q.astype(jnp.float32).reshape(batch, heads, nc, CHUNK, dk)
    kf = k.astype(jnp.float32).reshape(batch, heads, nc, CHUNK, dk)
    vf = v.astype(jnp.float32).reshape(batch, heads, nc, CHUNK, dv)
    lg = log_gamma.astype(jnp.float32)[:, None]
    idx = jnp.arange(CHUNK, dtype=jnp.float32)
    rel = idx[:, None] - idx[None, :]
    intra_decay = jnp.where(rel >= 0, jnp.exp(lg[:, :, None] * jnp.maximum(rel, 0.0)), 0.0)
    scores = jnp.einsum('bhnid,bhnjd->bhnij', qf, kf) * intra_decay[None, :, None]
    intra = jnp.einsum('bhnij,bhnje->bhnie', scores, vf)
    k_weight = jnp.exp(lg * (CHUNK - 1 - idx))
    chunk_kv = jnp.einsum('bhnjd,bhnje->bhnde', kf * k_weight[None, :, None, :, None], vf)
    chunk_decay = jnp.exp(lg[:, 0] * CHUNK)[None, :, None, None]

    def step(state, kv_c):
        return chunk_decay * state + kv_c, state

    _, prev_states = lax.scan(step, s0, jnp.moveaxis(chunk_kv, 2, 0))
    q_weight = jnp.exp(lg * (idx + 1.0))
    cross = jnp.einsum('bhnid,nbhde->bhnie', qf * q_weight[None, :, None, :, None], prev_states)
    return (intra + cross).reshape(batch, heads, n, dv)


def bidir_retention(q, k, v, log_decay, s0_f, s0_b):
    flip = lambda t: jnp.flip(t, axis=2)
    fwd = retention_chunked(q, k, v, log_decay[0], s0_f)
    bwd = retention_chunked(flip(q), flip(k), flip(v), log_decay[1], s0_b)
    return fwd + flip(bwd)


def context_states(k, v, log_decay):
    length = k.shape[2]
    pos = jnp.arange(length, dtype=jnp.float32)
    lg = log_decay.astype(jnp.float32)
    w_f = jnp.exp(lg[0][:, None] * (length - 1 - pos))
    w_b = jnp.exp(lg[1][:, None] * pos)
    kf, vf = k.astype(jnp.float32), v.astype(jnp.float32)
    s_f = jnp.einsum('bhld,bhle->bhde', kf * w_f[None, :, :, None], vf)
    s_b = jnp.einsum('bhld,bhle->bhde', kf * w_b[None, :, :, None], vf)
    return s_f, s_b


def short_conv(u, w):
    up = jnp.pad(u, ((0, 0), (1, 1), (0, 0)))
    return w[0] * up[:, :-2] + w[1] * up[:, 1:-1] + w[2] * up[:, 2:]


def mixer_merge(o_ret, g, conv_b, conv_c, conv_x, gate_ret, gate_conv, conv_w, w_ret_out, w_conv_out, w_out):
    b, h, n, dv = o_ret.shape
    o = head_rmsnorm(o_ret).transpose(0, 2, 1, 3).reshape(b, n, h * dv).astype(g.dtype)
    y_ret = (jax.nn.silu(g) * o) @ w_ret_out
    y_conv = (conv_b * short_conv(conv_c * conv_x, conv_w)) @ w_conv_out
    merged = jax.nn.sigmoid(gate_ret) * y_ret + jax.nn.sigmoid(gate_conv) * y_conv
    return merged @ w_out


def setup_inputs(seed: int = 0) -> dict:
    key = jax.random.key(seed)
    ks = jax.random.split(key, 20)

    def nrm(k, shape, scale=1.0):
        return jax.random.normal(k, shape, jnp.float32) * scale

    base_decay = jnp.log1p(-jnp.power(2.0, -5.0 - jnp.arange(RET_HEADS, dtype=jnp.float32)))
    return {
        "x": nrm(ks[0], (BATCH, SEQ, D_MODEL)),
        "c": nrm(ks[1], (BATCH, D_MODEL)),
        "ctx": nrm(ks[2], (BATCH, CTX_LEN, D_MODEL)),
        "c_ctx": nrm(ks[3], (D_MODEL,)),
        "norm_w": 1.0 + nrm(ks[4], (DEPTH, 3, D_MODEL), 0.05),
        "w_mod": nrm(ks[5], (DEPTH, D_MODEL, N_MOD * D_MODEL), 0.3 * D_MODEL ** -0.5),
        "b_mod": nrm(ks[6], (DEPTH, N_MOD * D_MODEL), 0.02),
        "ffn1_w_up": nrm(ks[7], (DEPTH, D_MODEL, 2 * D_FF), D_MODEL ** -0.5),
        "ffn1_w_down": nrm(ks[8], (DEPTH, D_FF, D_MODEL), D_FF ** -0.5),
        "w_in": nrm(ks[9], (DEPTH, D_MODEL, IN_WIDTH), D_MODEL ** -0.5),
        "ret_log_decay": base_decay[None, None, :] * (1.0 + nrm(ks[10], (DEPTH, 2, RET_HEADS), 0.05)),
        "conv_w": nrm(ks[11], (DEPTH, CONV_TAPS, CONV_WIDTH), CONV_TAPS ** -0.5),
        "w_ret_out": nrm(ks[12], (DEPTH, RET_V_WIDTH, D_MODEL), RET_V_WIDTH ** -0.5),
        "w_conv_out": nrm(ks[13], (DEPTH, CONV_WIDTH, D_MODEL), CONV_WIDTH ** -0.5),
        "w_out": nrm(ks[14], (DEPTH, D_MODEL, D_MODEL), D_MODEL ** -0.5),
        "ffn2_w_up": nrm(ks[15], (DEPTH, D_MODEL, 2 * D_FF), D_MODEL ** -0.5),
        "ffn2_w_down": nrm(ks[16], (DEPTH, D_FF, D_MODEL), D_FF ** -0.5),
        "final_norm_w": 1.0 + nrm(ks[17], (D_MODEL,), 0.05),
    }


def reference(x, c, ctx, c_ctx, norm_w, w_mod, b_mod, ffn1_w_up, ffn1_w_down, w_in, ret_log_decay,
              conv_w, w_ret_out, w_conv_out, w_out, ffn2_w_up, ffn2_w_down, final_norm_w):
    batch, n_lat, _ = x.shape
    ROWS = n_lat // GRID_W
    pos_row = jnp.repeat(jnp.arange(ROWS), GRID_W)
    pos_col = jnp.tile(jnp.arange(GRID_W), ROWS)
    zero_state = jnp.zeros((batch, RET_HEADS, RET_QK_HEAD, RET_V_HEAD), jnp.float32)
    k_scale = RET_QK_HEAD ** -0.5

    for l in range(DEPTH):
        last = l == DEPTH - 1
        mod_x = adaln_params(c, w_mod[l], b_mod[l])
        mod_c = adaln_params(c_ctx[None], w_mod[l], b_mod[l])

        x = ffn_sublayer(x, mod_x, 0, norm_w[l, 0], ffn1_w_up[l], ffn1_w_down[l])
        ctx = ffn_sublayer(ctx, mod_c, 0, norm_w[l, 0], ffn1_w_up[l], ffn1_w_down[l])

        yx = modulated_norm(x, norm_w[l, 1], mod_x, 1)
        yc = modulated_norm(ctx, norm_w[l, 1], mod_c, 1)
        px = split_projection(yx @ w_in[l])
        if last:
            kc_raw, vc_raw = jnp.split(yc @ w_in[l][:, :KV_WIDTH], [RET_QK_WIDTH], axis=-1)
        else:
            pc = split_projection(yc @ w_in[l])
            kc_raw, vc_raw = pc[0], pc[1]
        kc = to_heads(kc_raw) * k_scale
        vc = to_heads(vc_raw)
        s_f, s_b = context_states(kc, vc, ret_log_decay[l])

        kx = axial_rope(to_heads(px[0]) * k_scale, pos_row, pos_col)
        vx = to_heads(px[1])
        qx = axial_rope(to_heads(px[2]), pos_row, pos_col)
        o_x = bidir_retention(qx, kx, vx, ret_log_decay[l], s_f, s_b)
        x = x + mod_x[:, :, 5] * mixer_merge(o_x, *px[3:], conv_w[l], w_ret_out[l], w_conv_out[l], w_out[l])
        if not last:
            qc = to_heads(pc[2])
            o_c = bidir_retention(qc, kc, vc, ret_log_decay[l], zero_state, zero_state)
            ctx = ctx + mod_c[:, :, 5] * mixer_merge(o_c, *pc[3:], conv_w[l], w_ret_out[l], w_conv_out[l], w_out[l])

        x = ffn_sublayer(x, mod_x, 2, norm_w[l, 2], ffn2_w_up[l], ffn2_w_down[l])
        if not last:
            ctx = ffn_sublayer(ctx, mod_c, 2, norm_w[l, 2], ffn2_w_up[l], ffn2_w_down[l])

    return rmsnorm(x, final_norm_w)
```

```python
import functools

import jax
import jax.numpy as jnp
from jax import lax
from jax.experimental import pallas as pl
from jax.experimental.pallas import tpu as pltpu

D_MODEL = 1024
DEPTH = 4
GRID_W = 64
HEADS = 4
DK = 128
DV = 256
QK_W = HEADS * DK
V_W = HEADS * DV
KV_W = QK_W + V_W
IN_W = 8192
REST_W = IN_W - KV_W
D_FF = 2816
N_MOD = 9
ROPE_BASE = 10000.0
EPS = 1e-6

CHUNK = 256
HALO = 16
FF_SPLIT = 2
V7X_VMEM_BYTES = 64 * 1024 * 1024
VMEM_LIMIT = V7X_VMEM_BYTES - 8 * 1024 * 1024

Q0, G0, CB0, CC0, CX0, GR0, GC0 = 0, 512, 1536, 2560, 3584, 4608, 5632

BF16 = jnp.bfloat16
F32 = jnp.float32


def _dot(a, b):
    return jnp.dot(a, b, preferred_element_type=F32)


def _dot_nt(a, b):
    return lax.dot_general(a, b, (((1,), (1,)), ((), ())), preferred_element_type=F32)


def _dot_tn(a, b):
    return lax.dot_general(a, b, (((0,), (0,)), ((), ())), preferred_element_type=F32)


def _sigmoid(x):
    return 1.0 / (1.0 + jnp.exp(-x))


def _row_exp(s):
    return jnp.exp(jnp.full((1, DV), s, F32))


def _modnorm(x, nw, mod):
    ms = jnp.mean(x * x, axis=-1, keepdims=True)
    xn = x * lax.rsqrt(ms + EPS)
    return (xn * nw) * (1.0 + mod[1:2, :]) + mod[0:1, :]


def _rope(t, cos, sins):
    lane = lax.broadcasted_iota(jnp.int32, t.shape, 1)
    first = (lane % 64) < 32
    partner = jnp.where(first, pltpu.roll(t, DK - 32, 1), pltpu.roll(t, 32, 1))
    return t * cos + partner * sins


def _full(shape):
    return pl.BlockSpec(shape, lambda i: (0,) * len(shape))


def _resident(shape):
    return pl.BlockSpec(shape, lambda i: (0,) * len(shape), pipeline_mode=pl.Buffered(1))


def _params():
    return pltpu.CompilerParams(dimension_semantics=("arbitrary",), vmem_limit_bytes=VMEM_LIMIT)


def _mod_kernel(c_ref, w_ref, b_ref, o_ref):
    cv = c_ref[...]
    s = cv * _sigmoid(cv)
    o_ref[0] = jnp.dot(s, w_ref[0], preferred_element_type=F32, precision=lax.Precision.HIGHEST) + b_ref[0]


def _mod_call(cc, w_mod, b_mod):
    tn = 2304
    return pl.pallas_call(
        _mod_kernel,
        grid=(DEPTH, N_MOD * D_MODEL // tn),
        in_specs=[
            pl.BlockSpec((8, D_MODEL), lambda l, j: (0, 0)),
            pl.BlockSpec((1, D_MODEL, tn), lambda l, j: (l, 0, j)),
            pl.BlockSpec((1, 1, tn), lambda l, j: (l, 0, j)),
        ],
        out_specs=pl.BlockSpec((1, 8, tn), lambda l, j: (l, 0, j)),
        out_shape=jax.ShapeDtypeStruct((DEPTH, 8, N_MOD * D_MODEL), F32),
        compiler_params=pltpu.CompilerParams(dimension_semantics=("arbitrary", "arbitrary"),
                                             vmem_limit_bytes=VMEM_LIMIT),
        name="adaln_mod",
    )(cc, w_mod, b_mod.reshape(DEPTH, 1, N_MOD * D_MODEL))


def _ffn_kernel(x_ref, mod_ref, nw_ref, wup_ref, wdn_ref, fw_ref, o_ref, *, final_norm):
    x = x_ref[...]
    mod = mod_ref[...]
    y = _modnorm(x, nw_ref[...], mod).astype(BF16)
    fc = D_FF // FF_SPLIT
    acc = None
    for j in range(FF_SPLIT):
        a = _dot(y, wup_ref[:, j * fc:(j + 1) * fc])
        b = _dot(y, wup_ref[:, D_FF + j * fc:D_FF + (j + 1) * fc])
        hmid = (a * _sigmoid(a) * b).astype(BF16)
        part = _dot(hmid, wdn_ref[j * fc:(j + 1) * fc, :])
        acc = part if acc is None else acc + part
    out = x + (0.5 * mod[2:3, :]) * acc
    if final_norm:
        ms = jnp.mean(out * out, axis=-1, keepdims=True)
        out = out * lax.rsqrt(ms + EPS) * fw_ref[...]
    o_ref[...] = out


def _ffn_call(h, mod, nw, wup, wdn, fw, tm, final_norm):
    n = h.shape[0]
    return pl.pallas_call(
        functools.partial(_ffn_kernel, final_norm=final_norm),
        grid=(n // tm,),
        in_specs=[
            pl.BlockSpec((tm, D_MODEL), lambda i: (i, 0)),
            _full((3, D_MODEL)),
            _full((1, D_MODEL)),
            _resident((D_MODEL, 2 * D_FF)),
            _resident((D_FF, D_MODEL)),
            _full((1, D_MODEL)),
        ],
        out_specs=pl.BlockSpec((tm, D_MODEL), lambda i: (i, 0)),
        out_shape=jax.ShapeDtypeStruct((n, D_MODEL), F32),
        compiler_params=_params(),
        name="ffn",
    )(h, mod, nw, wup, wdn, fw)


def _head_table(dst_ref, lg_ref, d, expo):
    for h in range(HEADS):
        dst_ref[:, h * DK:(h + 1) * DK] = jnp.exp(lg_ref[d, h] * expo)


def _kv_kernel(lg_ref, x_ref, mod_ref, nw_ref, wkv_ref, cos_ref, sin_ref, s0_ref,
               k_ref, v_ref, sf_ref, sfin_ref, sbtot_ref,
               s_scr, sb_scr, wf_tab, wb_tab, *, n_chunks, with_bwd_total):
    i = pl.program_id(0)
    tm = x_ref.shape[0]

    @pl.when(i == 0)
    def _():
        s_scr[...] = s0_ref[...]
        sb_scr[...] = jnp.zeros_like(sb_scr)
        j = lax.broadcasted_iota(jnp.int32, (CHUNK, DK), 0).astype(F32)
        _head_table(wf_tab, lg_ref, 0, (CHUNK - 1.0) - j)
        _head_table(wb_tab, lg_ref, 1, j)

    y = _modnorm(x_ref[...], nw_ref[...], mod_ref[...]).astype(BF16)
    kv = _dot(y, wkv_ref[...])
    cos = cos_ref[...]
    sins = sin_ref[...]
    k_scale = DK ** -0.5
    kh = [_rope(kv[:, h * DK:(h + 1) * DK] * k_scale, cos, sins) for h in range(HEADS)]
    vb = kv[:, QK_W:].astype(BF16)
    v_ref[...] = vb
    for h in range(HEADS):
        k_ref[:, h * DK:(h + 1) * DK] = kh[h].astype(BF16)

    for c in range(n_chunks):
        rows = slice(c * CHUNK, (c + 1) * CHUNK)
        for h in range(HEADS):
            sf_ref[c, h] = s_scr[h].astype(BF16)
            vh = vb[rows, h * DV:(h + 1) * DV]
            kw = (kh[h][rows] * wf_tab[:, h * DK:(h + 1) * DK]).astype(BF16)
            s_scr[h] = _row_exp(lg_ref[0, h] * CHUNK) * s_scr[h] + _dot_tn(kw, vh)
            if with_bwd_total:
                kwb = (kh[h][rows] * wb_tab[:, h * DK:(h + 1) * DK]).astype(BF16)
                off = (i * tm + c * CHUNK).astype(F32)
                sb_scr[h] = sb_scr[h] + _row_exp(lg_ref[1, h] * off) * _dot_tn(kwb, vh)

    @pl.when(i == pl.num_programs(0) - 1)
    def _():
        sfin_ref[...] = s_scr[...]
        sbtot_ref[...] = sb_scr[...]


def _kv_call(lg, h, mod, nw, wkv, cos, sins, s0, tm, with_bwd_total):
    n = h.shape[0]
    n_chunks = tm // CHUNK
    state = (HEADS, DK, DV)
    return pl.pallas_call(
        functools.partial(_kv_kernel, n_chunks=n_chunks, with_bwd_total=with_bwd_total),
        grid=(n // tm,),
        in_specs=[
            pl.BlockSpec(memory_space=pltpu.SMEM),
            pl.BlockSpec((tm, D_MODEL), lambda i: (i, 0)),
            _full((3, D_MODEL)),
            _full((1, D_MODEL)),
            _resident((D_MODEL, KV_W)),
            pl.BlockSpec((tm, DK), lambda i: (i, 0)),
            pl.BlockSpec((tm, DK), lambda i: (i, 0)),
            _full(state),
        ],
        out_specs=[
            pl.BlockSpec((tm, QK_W), lambda i: (i, 0)),
            pl.BlockSpec((tm, V_W), lambda i: (i, 0)),
            pl.BlockSpec((n_chunks,) + state, lambda i: (i, 0, 0, 0)),
            _full(state),
            _full(state),
        ],
        out_shape=[
            jax.ShapeDtypeStruct((n, QK_W), BF16),
            jax.ShapeDtypeStruct((n, V_W), BF16),
            jax.ShapeDtypeStruct((n // CHUNK,) + state, BF16),
            jax.ShapeDtypeStruct(state, F32),
            jax.ShapeDtypeStruct(state, F32),
        ],
        scratch_shapes=[
            pltpu.VMEM(state, F32),
            pltpu.VMEM(state, F32),
            pltpu.VMEM((CHUNK, QK_W), F32),
            pltpu.VMEM((CHUNK, QK_W), F32),
        ],
        compiler_params=_params(),
        name="mixer_kv",
    )(lg, h, mod, nw, wkv, cos, sins, s0)


def _mix_kernel(lg_ref, x_ref, xp_ref, xn_ref, mod_ref, nw_ref, k_ref, v_ref, sf_ref, cos_ref, sin_ref,
                sb0_ref, cw_ref, wr_ref, wro_ref, wco_ref, wo_ref, o_ref,
                sb_scr, m_tab, qf_tab, qb_tab, wb_tab, ybuf, ubuf, obuf, *, n_chunks):
    i = pl.program_id(0)
    nt = pl.num_programs(0)
    tm = x_ref.shape[0]

    @pl.when(i == 0)
    def _():
        sb_scr[...] = sb0_ref[...]
        j = lax.broadcasted_iota(jnp.int32, (CHUNK, DK), 0).astype(F32)
        _head_table(qf_tab, lg_ref, 0, j + 1.0)
        _head_table(qb_tab, lg_ref, 1, CHUNK - j)
        _head_table(wb_tab, lg_ref, 1, j)
        r = lax.broadcasted_iota(jnp.int32, (CHUNK, CHUNK), 0)
        cidx = lax.broadcasted_iota(jnp.int32, (CHUNK, CHUNK), 1)
        rel = (r - cidx).astype(F32)
        for h in range(HEADS):
            fwd = jnp.exp(lg_ref[0, h] * jnp.maximum(rel, 0.0))
            bwd = jnp.exp(lg_ref[1, h] * jnp.maximum(-rel, 0.0))
            m_tab[h] = jnp.where(rel > 0, fwd, jnp.where(rel < 0, bwd, 2.0))

    x = x_ref[...]
    mod = mod_ref[...]
    nw = nw_ref[...]
    prev_ok = (i < nt - 1).astype(F32)
    next_ok = (i > 0).astype(F32)
    ybuf[0:HALO, :] = _modnorm(xp_ref[...], nw, mod).astype(BF16)
    ybuf[HALO:HALO + tm, :] = _modnorm(x, nw, mod).astype(BF16)
    ybuf[HALO + tm:, :] = _modnorm(xn_ref[...], nw, mod).astype(BF16)
    y = ybuf[HALO:HALO + tm, :]

    yall = ybuf[...]
    ubuf[...] = _dot(yall, wr_ref[:, CC0:CC0 + D_MODEL]) * _dot(yall, wr_ref[:, CX0:CX0 + D_MODEL])
    ubuf[0:HALO, :] = ubuf[0:HALO, :] * prev_ok
    ubuf[HALO + tm:, :] = ubuf[HALO + tm:, :] * next_ok
    cw = cw_ref[...]
    conv = (cw[0:1, :] * ubuf[HALO - 1:HALO - 1 + tm, :] + cw[1:2, :] * ubuf[HALO:HALO + tm, :]
            + cw[2:3, :] * ubuf[HALO + 1:HALO + 1 + tm, :])
    z = (_dot(y, wr_ref[:, CB0:CB0 + D_MODEL]) * conv).astype(BF16)
    merged = _sigmoid(_dot(y, wr_ref[:, GC0:GC0 + D_MODEL])) * _dot(z, wco_ref[...])

    q = _dot(y, wr_ref[:, Q0:Q0 + QK_W])
    cos = cos_ref[...]
    sins = sin_ref[...]
    for h in range(HEADS):
        qh_all = _rope(q[:, h * DK:(h + 1) * DK], cos, sins)
        decay_b = _row_exp(lg_ref[1, h] * CHUNK)
        for c in reversed(range(n_chunks)):
            rows = slice(c * CHUNK, (c + 1) * CHUNK)
            qh = qh_all[rows]
            kh = k_ref[rows, h * DK:(h + 1) * DK]
            vh = v_ref[rows, h * DV:(h + 1) * DV]
            p = (_dot_nt(qh.astype(BF16), kh) * m_tab[h]).astype(BF16)
            o = _dot(p, vh)
            o = o + _dot((qh * qf_tab[:, h * DK:(h + 1) * DK]).astype(BF16), sf_ref[c, h])
            o = o + _dot((qh * qb_tab[:, h * DK:(h + 1) * DK]).astype(BF16), sb_scr[h].astype(BF16))
            kwb = (kh.astype(F32) * wb_tab[:, h * DK:(h + 1) * DK]).astype(BF16)
            sb_scr[h] = decay_b * sb_scr[h] + _dot_tn(kwb, vh)
            o = o * lax.rsqrt(jnp.mean(o * o, axis=-1, keepdims=True) + EPS)
            obuf[rows, h * DV:(h + 1) * DV] = o
    g = _dot(y, wr_ref[:, G0:G0 + V_W])
    og = (g * _sigmoid(g) * obuf[...]).astype(BF16)
    merged = merged + _sigmoid(_dot(y, wr_ref[:, GR0:GR0 + D_MODEL])) * _dot(og, wro_ref[...])

    o_ref[...] = x + mod[2:3, :] * _dot(merged.astype(BF16), wo_ref[...])


def _mix_call(lg, h, mod, nw, k, v, sf, cos, sins, sb0, cw, wr, wro, wco, wo, tm):
    n = h.shape[0]
    nt = n // tm
    n_chunks = tm // CHUNK
    state = (HEADS, DK, DV)
    hb = tm // HALO
    last_halo = n // HALO - 1
    rev = lambda i: nt - 1 - i
    return pl.pallas_call(
        functools.partial(_mix_kernel, n_chunks=n_chunks),
        grid=(nt,),
        in_specs=[
            pl.BlockSpec(memory_space=pltpu.SMEM),
            pl.BlockSpec((tm, D_MODEL), lambda i: (rev(i), 0)),
            pl.BlockSpec((HALO, D_MODEL), lambda i: (jnp.maximum(rev(i) * hb - 1, 0), 0)),
            pl.BlockSpec((HALO, D_MODEL), lambda i: (jnp.minimum((rev(i) + 1) * hb, last_halo), 0)),
            _full((3, D_MODEL)),
            _full((1, D_MODEL)),
            pl.BlockSpec((tm, QK_W), lambda i: (rev(i), 0)),
            pl.BlockSpec((tm, V_W), lambda i: (rev(i), 0)),
            pl.BlockSpec((n_chunks,) + state, lambda i: (rev(i), 0, 0, 0)),
            pl.BlockSpec((tm, DK), lambda i: (rev(i), 0)),
            pl.BlockSpec((tm, DK), lambda i: (rev(i), 0)),
            _full(state),
            _full((3, D_MODEL)),
            _resident((D_MODEL, REST_W)),
            _resident((V_W, D_MODEL)),
            _resident((D_MODEL, D_MODEL)),
            _resident((D_MODEL, D_MODEL)),
        ],
        out_specs=pl.BlockSpec((tm, D_MODEL), lambda i: (rev(i), 0)),
        out_shape=jax.ShapeDtypeStruct((n, D_MODEL), F32),
        scratch_shapes=[
            pltpu.VMEM(state, F32),
            pltpu.VMEM((HEADS, CHUNK, CHUNK), F32),
            pltpu.VMEM((CHUNK, QK_W), F32),
            pltpu.VMEM((CHUNK, QK_W), F32),
            pltpu.VMEM((CHUNK, QK_W), F32),
            pltpu.VMEM((tm + 2 * HALO, D_MODEL), BF16),
            pltpu.VMEM((tm + 2 * HALO, D_MODEL), F32),
            pltpu.VMEM((tm, V_W), F32),
        ],
        compiler_params=_params(),
        name="mixer_out",
    )(lg, h, h, h, mod, nw, k, v, sf, cos, sins, sb0, cw, wr, wro, wco, wo)


def _rope_tables(n):
    pos = jnp.arange(n)
    pos_row = (pos // GRID_W).astype(F32)
    pos_col = (pos % GRID_W).astype(F32)
    n_freq = DK // 4
    inv_freq = ROPE_BASE ** (-jnp.arange(n_freq, dtype=F32) / n_freq)
    ang_r = pos_row[:, None] * inv_freq[None, :]
    ang_c = pos_col[:, None] * inv_freq[None, :]
    cos = jnp.concatenate([jnp.cos(ang_r)] * 2 + [jnp.cos(ang_c)] * 2, axis=-1)
    sins = jnp.concatenate([-jnp.sin(ang_r), jnp.sin(ang_r), -jnp.sin(ang_c), jnp.sin(ang_c)], axis=-1)
    return cos, sins


def kernel(x, c, ctx, c_ctx, norm_w, w_mod, b_mod, ffn1_w_up, ffn1_w_down, w_in, ret_log_decay, conv_w,
           w_ret_out, w_conv_out, w_out, ffn2_w_up, ffn2_w_down, final_norm_w):
    assert x.shape[0] == 1 and ctx.shape[0] == 1
    n_lat, n_ctx = x.shape[1], ctx.shape[1]
    tm_x, tm_c = 512, n_ctx
    assert n_lat % tm_x == 0 and n_ctx % CHUNK == 0

    hx, hc = x[0], ctx[0]
    cc = jnp.zeros((8, D_MODEL), F32).at[0].set(c[0]).at[1].set(c_ctx)
    mods = _mod_call(cc, w_mod, b_mod).reshape(DEPTH, 8, N_MOD, D_MODEL)

    cos_x, sin_x = _rope_tables(n_lat)
    cos_c, sin_c = jnp.ones((n_ctx, DK), F32), jnp.zeros((n_ctx, DK), F32)
    zero_state = jnp.zeros((HEADS, DK, DV), F32)
    fw = final_norm_w.reshape(1, D_MODEL)

    for l in range(DEPTH):
        last = l == DEPTH - 1
        mx, mc = mods[l, 0], mods[l, 1]
        nw = norm_w[l].reshape(3, 1, D_MODEL)
        lg = ret_log_decay[l]
        up1, dn1 = ffn1_w_up[l].astype(BF16), ffn1_w_down[l].astype(BF16)
        up2, dn2 = ffn2_w_up[l].astype(BF16), ffn2_w_down[l].astype(BF16)
        wkv, wr = w_in[l][:, :KV_W].astype(BF16), w_in[l][:, KV_W:].astype(BF16)
        wro, wco, wo = w_ret_out[l].astype(BF16), w_conv_out[l].astype(BF16), w_out[l].astype(BF16)
        cw = conv_w[l]

        hx = _ffn_call(hx, mx[0:3], nw[0], up1, dn1, fw, tm_x, False)
        hc = _ffn_call(hc, mc[0:3], nw[0], up1, dn1, fw, tm_c, False)

        kc, vc, sfc, s_f, s_b = _kv_call(lg, hc, mc[3:6], nw[1], wkv, cos_c, sin_c, zero_state, tm_c, True)
        kx, vx, sfx, _, _ = _kv_call(lg, hx, mx[3:6], nw[1], wkv, cos_x, sin_x, s_f, tm_x, False)
        hx = _mix_call(lg, hx, mx[3:6], nw[1], kx, vx, sfx, cos_x, sin_x, s_b, cw, wr, wro, wco, wo, tm_x)
        if not last:
            hc = _mix_call(lg, hc, mc[3:6], nw[1], kc, vc, sfc, cos_c, sin_c, zero_state, cw, wr, wro, wco, wo,
                           tm_c)

        hx = _ffn_call(hx, mx[6:9], nw[2], up2, dn2, fw, tm_x, last)
        if not last:
            hc = _ffn_call(hc, mc[6:9], nw[2], up2, dn2, fw, tm_c, False)

    return hx[None]
```

```python
import functools

import jax
import jax.numpy as jnp
from jax import lax
from jax.experimental import pallas as pl
from jax.experimental.pallas import tpu as pltpu

D_MODEL = 1024
DEPTH = 4
GRID_W = 64
HEADS = 4
DK = 128
DV = 256
QK_W = HEADS * DK
V_W = HEADS * DV
KV_W = QK_W + V_W
IN_W = 8192
D_FF = 2816
N_MOD = 9
ROPE_BASE = 10000.0
EPS = 1e-6

CHUNK = 256
HALO = 16
V7X_MXU_DIM = 256
FF_GROUPS = ((0, 6 * V7X_MXU_DIM), (6 * V7X_MXU_DIM, D_FF))
V7X_VMEM_BYTES = 64 * 1024 * 1024
VMEM_LIMIT = V7X_VMEM_BYTES - 8 * 1024 * 1024

Q0, G0, CB0, CC0, CX0, GR0, GC0 = 1536, 2048, 3072, 4096, 5120, 6144, 7168

BF16 = jnp.bfloat16
F32 = jnp.float32


def _dot(a, b):
    return jnp.dot(a, b, preferred_element_type=F32)


def _dot_nt(a, b):
    return lax.dot_general(a, b, (((1,), (1,)), ((), ())), preferred_element_type=F32)


def _dot_tn(a, b):
    return lax.dot_general(a, b, (((0,), (0,)), ((), ())), preferred_element_type=F32)


def _sigmoid(x):
    return 1.0 / (1.0 + jnp.exp(-x))


def _row_exp(s):
    return jnp.exp(jnp.full((1, DV), s, F32))


def _modnorm(x, nw, mod):
    ms = jnp.mean(x * x, axis=-1, keepdims=True)
    xn = x * lax.rsqrt(ms + EPS)
    return (xn * nw) * (1.0 + mod[1:2, :]) + mod[0:1, :]


def _rope(t, cos, sins):
    lane = lax.broadcasted_iota(jnp.int32, t.shape, 1)
    first = (lane % 64) < 32
    partner = jnp.where(first, pltpu.roll(t, DK - 32, 1), pltpu.roll(t, 32, 1))
    return t * cos + partner * sins


def _full(shape):
    return pl.BlockSpec(shape, lambda i: (0,) * len(shape))


def _layer(shape, idx, resident=False):
    spec_shape = (None,) + tuple(shape)
    index_map = lambda i: (idx,) + (0,) * len(shape)
    if resident:
        return pl.BlockSpec(spec_shape, index_map, pipeline_mode=pl.Buffered(1))
    return pl.BlockSpec(spec_shape, index_map)


def _params():
    return pltpu.CompilerParams(dimension_semantics=("arbitrary",), vmem_limit_bytes=VMEM_LIMIT)


def _mod_kernel(c_ref, w_ref, b_ref, o_ref):
    cv = c_ref[...]
    s = cv * _sigmoid(cv)
    o_ref[0] = jnp.dot(s, w_ref[0], preferred_element_type=F32, precision=lax.Precision.HIGHEST) + b_ref[0]


def _mod_call(cc, w_mod, b_mod):
    tn = 2304
    return pl.pallas_call(
        _mod_kernel,
        grid=(DEPTH, N_MOD * D_MODEL // tn),
        in_specs=[
            pl.BlockSpec((8, D_MODEL), lambda l, j: (0, 0)),
            pl.BlockSpec((1, D_MODEL, tn), lambda l, j: (l, 0, j)),
            pl.BlockSpec((1, 1, tn), lambda l, j: (l, 0, j)),
        ],
        out_specs=pl.BlockSpec((1, 8, tn), lambda l, j: (l, 0, j)),
        out_shape=jax.ShapeDtypeStruct((DEPTH, 8, N_MOD * D_MODEL), F32),
        compiler_params=pltpu.CompilerParams(dimension_semantics=("arbitrary", "arbitrary"),
                                             vmem_limit_bytes=VMEM_LIMIT),
        name="adaln_mod",
    )(cc, w_mod, b_mod.reshape(DEPTH, 1, N_MOD * D_MODEL))


def _ffn_kernel(x_ref, mod_ref, nw_ref, wup_ref, wdn_ref, fw_ref, o_ref, *, final_norm):
    x = x_ref[...]
    mod = mod_ref[...]
    y = _modnorm(x, nw_ref[...], mod).astype(BF16)
    acc = None
    for lo, hi in FF_GROUPS:
        a = _dot(y, wup_ref[:, lo:hi])
        b = _dot(y, wup_ref[:, D_FF + lo:D_FF + hi])
        hmid = (a * _sigmoid(a) * b).astype(BF16)
        part = _dot(hmid, wdn_ref[lo:hi, :])
        acc = part if acc is None else acc + part
    out = x + (0.5 * mod[2:3, :]) * acc
    if final_norm:
        ms = jnp.mean(out * out, axis=-1, keepdims=True)
        out = out * lax.rsqrt(ms + EPS) * fw_ref[...]
    o_ref[...] = out


def _ffn_call(h, mods, nws, wup, wdn, fw, l, stream, sub, tm, final_norm):
    n = h.shape[0]
    return pl.pallas_call(
        functools.partial(_ffn_kernel, final_norm=final_norm),
        grid=(n // tm,),
        in_specs=[
            pl.BlockSpec((tm, D_MODEL), lambda i: (i, 0)),
            _layer((3, D_MODEL), (l * 8 + stream) * 3 + sub),
            _layer((1, D_MODEL), l * 3 + sub),
            _layer((D_MODEL, 2 * D_FF), l, resident=True),
            _layer((D_FF, D_MODEL), l, resident=True),
            _full((1, D_MODEL)),
        ],
        out_specs=pl.BlockSpec((tm, D_MODEL), lambda i: (i, 0)),
        out_shape=jax.ShapeDtypeStruct((n, D_MODEL), F32),
        compiler_params=_params(),
        name="ffn",
    )(h, mods, nws, wup, wdn, fw)


def _head_table(dst_ref, lg_ref, l, d, expo):
    for h in range(HEADS):
        dst_ref[:, h * DK:(h + 1) * DK] = jnp.exp(lg_ref[l, d, h] * expo)


def _kv_kernel(lg_ref, x_ref, mod_ref, nw_ref, wkv_ref, cos_ref, sin_ref, s0_ref,
               k_ref, v_ref, sf_ref, sfin_ref, sbtot_ref,
               s_scr, sb_scr, wf_tab, wb_tab, *, l, n_chunks, with_bwd_total):
    i = pl.program_id(0)
    tm = x_ref.shape[0]

    @pl.when(i == 0)
    def _():
        s_scr[...] = s0_ref[...]
        sb_scr[...] = jnp.zeros_like(sb_scr)
        j = lax.broadcasted_iota(jnp.int32, (CHUNK, DK), 0).astype(F32)
        _head_table(wf_tab, lg_ref, l, 0, (CHUNK - 1.0) - j)
        _head_table(wb_tab, lg_ref, l, 1, j)

    y = _modnorm(x_ref[...], nw_ref[...], mod_ref[...]).astype(BF16)
    kv = _dot(y, wkv_ref[...])
    cos = cos_ref[...]
    sins = sin_ref[...]
    k_scale = DK ** -0.5
    kh = [_rope(kv[:, h * DK:(h + 1) * DK] * k_scale, cos, sins) for h in range(HEADS)]
    vb = kv[:, QK_W:].astype(BF16)
    v_ref[...] = vb
    for h in range(HEADS):
        k_ref[:, h * DK:(h + 1) * DK] = kh[h].astype(BF16)

    for c in range(n_chunks):
        rows = slice(c * CHUNK, (c + 1) * CHUNK)
        for h in range(HEADS):
            sf_ref[c, h] = s_scr[h].astype(BF16)
            vh = vb[rows, h * DV:(h + 1) * DV]
            kw = (kh[h][rows] * wf_tab[:, h * DK:(h + 1) * DK]).astype(BF16)
            s_scr[h] = _row_exp(lg_ref[l, 0, h] * CHUNK) * s_scr[h] + _dot_tn(kw, vh)
            if with_bwd_total:
                kwb = (kh[h][rows] * wb_tab[:, h * DK:(h + 1) * DK]).astype(BF16)
                off = (i * tm + c * CHUNK).astype(F32)
                sb_scr[h] = sb_scr[h] + _row_exp(lg_ref[l, 1, h] * off) * _dot_tn(kwb, vh)

    @pl.when(i == pl.num_programs(0) - 1)
    def _():
        sfin_ref[...] = s_scr[...]
        sbtot_ref[...] = sb_scr[...]


def _kv_call(lg, h, mods, nws, w_in, cos, sins, s0, l, stream, tm, with_bwd_total):
    n = h.shape[0]
    n_chunks = tm // CHUNK
    state = (HEADS, DK, DV)
    return pl.pallas_call(
        functools.partial(_kv_kernel, l=l, n_chunks=n_chunks, with_bwd_total=with_bwd_total),
        grid=(n // tm,),
        in_specs=[
            pl.BlockSpec(memory_space=pltpu.SMEM),
            pl.BlockSpec((tm, D_MODEL), lambda i: (i, 0)),
            _layer((3, D_MODEL), (l * 8 + stream) * 3 + 1),
            _layer((1, D_MODEL), l * 3 + 1),
            _layer((D_MODEL, KV_W), l, resident=True),
            pl.BlockSpec((tm, DK), lambda i: (i, 0)),
            pl.BlockSpec((tm, DK), lambda i: (i, 0)),
            _full(state),
        ],
        out_specs=[
            pl.BlockSpec((tm, QK_W), lambda i: (i, 0)),
            pl.BlockSpec((tm, V_W), lambda i: (i, 0)),
            pl.BlockSpec((n_chunks,) + state, lambda i: (i, 0, 0, 0)),
            _full(state),
            _full(state),
        ],
        out_shape=[
            jax.ShapeDtypeStruct((n, QK_W), BF16),
            jax.ShapeDtypeStruct((n, V_W), BF16),
            jax.ShapeDtypeStruct((n // CHUNK,) + state, BF16),
            jax.ShapeDtypeStruct(state, F32),
            jax.ShapeDtypeStruct(state, F32),
        ],
        scratch_shapes=[
            pltpu.VMEM(state, F32),
            pltpu.VMEM(state, F32),
            pltpu.VMEM((CHUNK, QK_W), F32),
            pltpu.VMEM((CHUNK, QK_W), F32),
        ],
        compiler_params=_params(),
        name="mixer_kv",
    )(lg, h, mods, nws, w_in, cos, sins, s0)


def _mix_kernel(lg_ref, x_ref, xp_ref, xn_ref, mod_ref, nw_ref, k_ref, v_ref, sf_ref, cos_ref, sin_ref,
                sb0_ref, cw_ref, win_ref, wro_ref, wco_ref, wo_ref, o_ref,
                sb_scr, m_tab, qf_tab, qb_tab, wb_tab, ybuf, ubuf, obuf, *, l, n_chunks):
    i = pl.program_id(0)
    nt = pl.num_programs(0)
    tm = x_ref.shape[0]

    @pl.when(i == 0)
    def _():
        sb_scr[...] = sb0_ref[...]
        j = lax.broadcasted_iota(jnp.int32, (CHUNK, DK), 0).astype(F32)
        _head_table(qf_tab, lg_ref, l, 0, j + 1.0)
        _head_table(qb_tab, lg_ref, l, 1, CHUNK - j)
        _head_table(wb_tab, lg_ref, l, 1, j)
        r = lax.broadcasted_iota(jnp.int32, (CHUNK, CHUNK), 0)
        cidx = lax.broadcasted_iota(jnp.int32, (CHUNK, CHUNK), 1)
        rel = (r - cidx).astype(F32)
        for h in range(HEADS):
            fwd = jnp.exp(lg_ref[l, 0, h] * jnp.maximum(rel, 0.0))
            bwd = jnp.exp(lg_ref[l, 1, h] * jnp.maximum(-rel, 0.0))
            m_tab[h] = jnp.where(rel > 0, fwd, jnp.where(rel < 0, bwd, 2.0))

    x = x_ref[...]
    mod = mod_ref[...]
    nw = nw_ref[...]
    prev_ok = (i < nt - 1).astype(F32)
    next_ok = (i > 0).astype(F32)
    ybuf[0:HALO, :] = _modnorm(xp_ref[...], nw, mod).astype(BF16)
    ybuf[HALO:HALO + tm, :] = _modnorm(x, nw, mod).astype(BF16)
    ybuf[HALO + tm:, :] = _modnorm(xn_ref[...], nw, mod).astype(BF16)
    y = ybuf[HALO:HALO + tm, :]

    yall = ybuf[...]
    ubuf[...] = _dot(yall, win_ref[:, CC0:CC0 + D_MODEL]) * _dot(yall, win_ref[:, CX0:CX0 + D_MODEL])
    ubuf[0:HALO, :] = ubuf[0:HALO, :] * prev_ok
    ubuf[HALO + tm:, :] = ubuf[HALO + tm:, :] * next_ok
    cw = cw_ref[...]
    conv = (cw[0:1, :] * ubuf[HALO - 1:HALO - 1 + tm, :] + cw[1:2, :] * ubuf[HALO:HALO + tm, :]
            + cw[2:3, :] * ubuf[HALO + 1:HALO + 1 + tm, :])
    z = (_dot(y, win_ref[:, CB0:CB0 + D_MODEL]) * conv).astype(BF16)
    merged = _sigmoid(_dot(y, win_ref[:, GC0:GC0 + D_MODEL])) * _dot(z, wco_ref[...])

    q = _dot(y, win_ref[:, Q0:Q0 + QK_W])
    cos = cos_ref[...]
    sins = sin_ref[...]
    for h in range(HEADS):
        qh_all = _rope(q[:, h * DK:(h + 1) * DK], cos, sins)
        decay_b = _row_exp(lg_ref[l, 1, h] * CHUNK)
        for c in reversed(range(n_chunks)):
            rows = slice(c * CHUNK, (c + 1) * CHUNK)
            qh = qh_all[rows]
            kh = k_ref[rows, h * DK:(h + 1) * DK]
            vh = v_ref[rows, h * DV:(h + 1) * DV]
            p = (_dot_nt(qh.astype(BF16), kh) * m_tab[h]).astype(BF16)
            o = _dot(p, vh)
            o = o + _dot((qh * qf_tab[:, h * DK:(h + 1) * DK]).astype(BF16), sf_ref[c, h])
            o = o + _dot((qh * qb_tab[:, h * DK:(h + 1) * DK]).astype(BF16), sb_scr[h].astype(BF16))
            kwb = (kh.astype(F32) * wb_tab[:, h * DK:(h + 1) * DK]).astype(BF16)
            sb_scr[h] = decay_b * sb_scr[h] + _dot_tn(kwb, vh)
            o = o * lax.rsqrt(jnp.mean(o * o, axis=-1, keepdims=True) + EPS)
            obuf[rows, h * DV:(h + 1) * DV] = o
    g = _dot(y, win_ref[:, G0:G0 + V_W])
    og = (g * _sigmoid(g) * obuf[...]).astype(BF16)
    merged = merged + _sigmoid(_dot(y, win_ref[:, GR0:GR0 + D_MODEL])) * _dot(og, wro_ref[...])

    o_ref[...] = x + mod[2:3, :] * _dot(merged.astype(BF16), wo_ref[...])


def _mix_call(lg, h, mods, nws, k, v, sf, cos, sins, sb0, cw, w_in, wro, wco, wo, l, stream, tm):
    n = h.shape[0]
    nt = n // tm
    n_chunks = tm // CHUNK
    state = (HEADS, DK, DV)
    hb = tm // HALO
    last_halo = n // HALO - 1
    rev = lambda i: nt - 1 - i
    return pl.pallas_call(
        functools.partial(_mix_kernel, l=l, n_chunks=n_chunks),
        grid=(nt,),
        in_specs=[
            pl.BlockSpec(memory_space=pltpu.SMEM),
            pl.BlockSpec((tm, D_MODEL), lambda i: (rev(i), 0)),
            pl.BlockSpec((HALO, D_MODEL), lambda i: (jnp.maximum(rev(i) * hb - 1, 0), 0)),
            pl.BlockSpec((HALO, D_MODEL), lambda i: (jnp.minimum((rev(i) + 1) * hb, last_halo), 0)),
            _layer((3, D_MODEL), (l * 8 + stream) * 3 + 1),
            _layer((1, D_MODEL), l * 3 + 1),
            pl.BlockSpec((tm, QK_W), lambda i: (rev(i), 0)),
            pl.BlockSpec((tm, V_W), lambda i: (rev(i), 0)),
            pl.BlockSpec((n_chunks,) + state, lambda i: (rev(i), 0, 0, 0)),
            pl.BlockSpec((tm, DK), lambda i: (rev(i), 0)),
            pl.BlockSpec((tm, DK), lambda i: (rev(i), 0)),
            _full(state),
            _layer((3, D_MODEL), l),
            _layer((D_MODEL, IN_W), l, resident=True),
            _layer((V_W, D_MODEL), l, resident=True),
            _layer((D_MODEL, D_MODEL), l, resident=True),
            _layer((D_MODEL, D_MODEL), l, resident=True),
        ],
        out_specs=pl.BlockSpec((tm, D_MODEL), lambda i: (rev(i), 0)),
        out_shape=jax.ShapeDtypeStruct((n, D_MODEL), F32),
        scratch_shapes=[
            pltpu.VMEM(state, F32),
            pltpu.VMEM((HEADS, CHUNK, CHUNK), F32),
            pltpu.VMEM((CHUNK, QK_W), F32),
            pltpu.VMEM((CHUNK, QK_W), F32),
            pltpu.VMEM((CHUNK, QK_W), F32),
            pltpu.VMEM((tm + 2 * HALO, D_MODEL), BF16),
            pltpu.VMEM((tm + 2 * HALO, D_MODEL), F32),
            pltpu.VMEM((tm, V_W), F32),
        ],
        compiler_params=_params(),
        name="mixer_out",
    )(lg, h, h, h, mods, nws, k, v, sf, cos, sins, sb0, cw, w_in, wro, wco, wo)


def _rope_tables(n):
    n_freq = DK // 4
    inv_freq = ROPE_BASE ** (-jnp.arange(n_freq, dtype=F32) / n_freq)
    rows = n // GRID_W
    ang_r = jnp.arange(rows).astype(F32)[:, None] * inv_freq[None, :]
    ang_c = jnp.arange(GRID_W).astype(F32)[:, None] * inv_freq[None, :]
    by_row = lambda t: jnp.broadcast_to(t[:, None, :], (rows, GRID_W, n_freq))
    by_col = lambda t: jnp.broadcast_to(t[None, :, :], (rows, GRID_W, n_freq))
    cr, sr, ccol, sc = jnp.cos(ang_r), jnp.sin(ang_r), jnp.cos(ang_c), jnp.sin(ang_c)
    cos = jnp.concatenate([by_row(cr), by_row(cr), by_col(ccol), by_col(ccol)], axis=-1)
    sins = jnp.concatenate([by_row(-sr), by_row(sr), by_col(-sc), by_col(sc)], axis=-1)
    return cos.reshape(n, DK), sins.reshape(n, DK)


def kernel(x, c, ctx, c_ctx, norm_w, w_mod, b_mod, ffn1_w_up, ffn1_w_down, w_in, ret_log_decay, conv_w,
           w_ret_out, w_conv_out, w_out, ffn2_w_up, ffn2_w_down, final_norm_w):
    assert x.shape[0] == 1 and ctx.shape[0] == 1
    n_lat, n_ctx = x.shape[1], ctx.shape[1]
    tm_x, tm_c = 512, n_ctx
    assert n_lat % tm_x == 0 and n_lat % GRID_W == 0 and n_ctx % CHUNK == 0

    hx, hc = x[0], ctx[0]
    cc = jnp.zeros((8, D_MODEL), F32).at[0].set(c[0]).at[1].set(c_ctx)
    mods = _mod_call(cc, w_mod, b_mod).reshape(DEPTH * 8 * 3, 3, D_MODEL)
    nws = norm_w.reshape(DEPTH * 3, 1, D_MODEL)

    cos_x, sin_x = _rope_tables(n_lat)
    cos_c, sin_c = jnp.ones((n_ctx, DK), F32), jnp.zeros((n_ctx, DK), F32)
    zero_state = jnp.zeros((HEADS, DK, DV), F32)
    fw = final_norm_w.reshape(1, D_MODEL)
    lg = ret_log_decay
    up1, dn1 = ffn1_w_up.astype(BF16), ffn1_w_down.astype(BF16)
    up2, dn2 = ffn2_w_up.astype(BF16), ffn2_w_down.astype(BF16)
    win = w_in.astype(BF16)
    wro, wco, wo = w_ret_out.astype(BF16), w_conv_out.astype(BF16), w_out.astype(BF16)

    for l in range(DEPTH):
        last = l == DEPTH - 1
        hx = _ffn_call(hx, mods, nws, up1, dn1, fw, l, 0, 0, tm_x, False)
        hc = _ffn_call(hc, mods, nws, up1, dn1, fw, l, 1, 0, tm_c, False)

        kc, vc, sfc, s_f, s_b = _kv_call(lg, hc, mods, nws, win, cos_c, sin_c, zero_state, l, 1, tm_c, True)
        kx, vx, sfx, _, _ = _kv_call(lg, hx, mods, nws, win, cos_x, sin_x, s_f, l, 0, tm_x, False)
        hx = _mix_call(lg, hx, mods, nws, kx, vx, sfx, cos_x, sin_x, s_b, conv_w, win, wro, wco, wo, l, 0, tm_x)
        if not last:
            hc = _mix_call(lg, hc, mods, nws, kc, vc, sfc, cos_c, sin_c, zero_state, conv_w, win, wro, wco, wo,
                           l, 1, tm_c)

        hx = _ffn_call(hx, mods, nws, up2, dn2, fw, l, 0, 2, tm_x, last)
        if not last:
            hc = _ffn_call(hc, mods, nws, up2, dn2, fw, l, 1, 2, tm_c, False)

    return hx[None]
```
